```python
import math
import jax, jax.numpy as jnp
from jax import lax
import numpy as np

D_MODEL = 1024
BATCH = 8
SEQ = 8192
DEPTH = 1
DEC_BATCH = 32
DEC_SEQ = 16
PAST_LEN = 2048

CHUNK = 64
Q_BLOCK = 128
DA_HEADS = 4
DA_HEAD_DIM = 64
DA_WIDTH = DA_HEADS * 2 * DA_HEAD_DIM
DN_HEADS = 4
DN_HEAD_DIM = 128
DN_WIDTH = DN_HEADS * DN_HEAD_DIM
CONV_WIDTH = 4
REL_BUCKETS = 32
REL_MAX_DIST = 128
IN_WIDTH = 4 * DA_WIDTH + 4 * DN_WIDTH + 2 * DN_HEADS + 2 * D_MODEL
DEEPNORM_ALPHA = (2.0 * DEPTH) ** 0.25
DEEPNORM_BETA = (8.0 * DEPTH) ** -0.25
LN_EPS = 1e-5
NORM_EPS = 1e-6

kernel_name = "diffattn_gdn_gated_merge_streaming_step"

F32 = jnp.float32


def _layernorm(x):
    xf = x.astype(F32)
    mu = jnp.mean(xf, axis=-1, keepdims=True)
    var = jnp.mean(jnp.square(xf - mu), axis=-1, keepdims=True)
    return (xf - mu) * lax.rsqrt(var + LN_EPS)


def _rmsnorm(x, w, eps):
    xf = x.astype(F32)
    return xf * lax.rsqrt(jnp.mean(xf * xf, axis=-1, keepdims=True) + eps) * w.astype(F32)


def _l2norm(x):
    return x * lax.rsqrt(jnp.sum(x * x, axis=-1, keepdims=True) + NORM_EPS)


def _rel_bucket(rel):
    nb = REL_BUCKETS // 2
    max_exact = nb // 2
    n = jnp.abs(rel)
    large = max_exact + (jnp.log(jnp.maximum(n, 1).astype(F32) / max_exact)
                         / math.log(REL_MAX_DIST / max_exact) * (nb - max_exact)).astype(jnp.int32)
    large = jnp.minimum(large, nb - 1)
    return jnp.where(rel > 0, nb, 0) + jnp.where(n < max_exact, n, large)


def _diff_attend(q, k, v, q_pos, k_pos, rel_table, lam):
    s = jnp.einsum('bqhcd,bkhcd->bchqk', q, k, preferred_element_type=F32) * (DA_HEAD_DIM ** -0.5)
    bias = rel_table.astype(F32)[_rel_bucket(k_pos[None, :] - q_pos[:, None])]
    s = s + jnp.transpose(bias, (2, 0, 1))[None, None]
    visible = (k_pos[None, :] // CHUNK) <= (q_pos[:, None] // CHUNK)
    p = jax.nn.softmax(jnp.where(visible, s, -jnp.inf), axis=-1)
    w = p[:, 0] - lam * p[:, 1]
    return jnp.einsum('bhqk,bkhe->bqhe', w.astype(v.dtype), v)


def _diff_attn_prompt(q, k, v, rel_table, lam):
    B, S = q.shape[:2]
    nb = S // Q_BLOCK
    qb = jnp.swapaxes(q.reshape((B, nb, Q_BLOCK) + q.shape[2:]), 0, 1)
    k_pos = jnp.arange(S, dtype=jnp.int32)

    def one(args):
        i, qi = args
        q_pos = i * Q_BLOCK + jnp.arange(Q_BLOCK, dtype=jnp.int32)
        return _diff_attend(qi, k, v, q_pos, k_pos, rel_table, lam)

    o = lax.map(one, (jnp.arange(nb, dtype=jnp.int32), qb))
    return jnp.swapaxes(o, 0, 1).reshape((B, S) + o.shape[3:])


def _diff_out(o, subln_w, lam_init, gate):
    B, L = o.shape[:2]
    o = (_rmsnorm(o, subln_w, LN_EPS) * (1.0 - lam_init)).reshape(B, L, DA_WIDTH)
    return o * jax.nn.silu(gate.astype(F32))


def _causal_conv(x, prev, w):
    xp = jnp.concatenate([prev, x], axis=1)
    L = x.shape[1]
    y = sum(xp[:, j:j + L] * w[j] for j in range(CONV_WIDTH))
    return jax.nn.silu(y), xp[:, xp.shape[1] - (CONV_WIDTH - 1):]


def _gdn_inputs(qkv, a, b, a_log, dt_bias):
    B, L, _ = qkv.shape
    heads = lambda t: t.reshape(B, L, DN_HEADS, DN_HEAD_DIM).transpose(0, 2, 1, 3)
    q, k, v = (heads(t) for t in jnp.split(qkv.astype(F32), 3, axis=-1))
    q = _l2norm(q) * (DN_HEAD_DIM ** -0.5)
    k = _l2norm(k)
    beta = jax.nn.sigmoid(b.astype(F32)).transpose(0, 2, 1)
    g = (-jnp.exp(a_log.astype(F32)) * jax.nn.softplus(a.astype(F32) + dt_bias.astype(F32))).transpose(0, 2, 1)
    return q, k, v, beta, g


def _gdn_prep(q, k, v, beta, g):
    C = g.shape[-1]
    G = jnp.cumsum(g, axis=-1)
    causal = jnp.tril(jnp.ones((C, C), dtype=bool))
    strict = jnp.tril(jnp.ones((C, C), dtype=bool), -1)
    decay = jnp.exp(jnp.where(causal, G[..., :, None] - G[..., None, :], -jnp.inf))
    kb = k * beta[..., None]
    lower = jnp.where(strict, jnp.einsum('...id,...jd->...ij', kb, k) * decay, 0.0)
    rhs = jnp.concatenate([v * beta[..., None], kb * jnp.exp(G)[..., None]], axis=-1)
    sol = lax.linalg.triangular_solve(lower, rhs, left_side=True, lower=True, unit_diagonal=True)
    u, w = sol[..., :DN_HEAD_DIM], sol[..., DN_HEAD_DIM:]
    qk = jnp.einsum('...id,...jd->...ij', q, k) * decay
    qg = q * jnp.exp(G)[..., None]
    kd = k * jnp.exp(G[..., -1:] - G)[..., None]
    gl = jnp.exp(G[..., -1])
    return u, w, qk, qg, kd, gl


def _gdn_step(s, u, w, qk, qg, kd, gl):
    v_new = u - jnp.einsum('...cd,...de->...ce', w, s)
    o = jnp.einsum('...cd,...de->...ce', qg, s) + jnp.einsum('...ij,...je->...ie', qk, v_new)
    s = s * gl[..., None, None] + jnp.einsum('...cd,...ce->...de', kd, v_new)
    return s, o


def _layer(x, c, p, rel_table, lam_init, cache):
    (w_ada, b_ada, w_in, lam_q1, lam_k1, lam_q2, lam_k2, subln_w, conv_w,
     a_log, dt_bias, dn_norm_w, w_pa, w_pb, w_out, ln_g, ln_b) = p
    dt = x.dtype
    B, L, _ = x.shape
    mod = jnp.dot(jax.nn.silu(c), w_ada) + b_ada
    shift, scale, gate = jnp.split(mod[:, None, :], 3, axis=-1)
    h = (_layernorm(x) * (1.0 + scale) + shift).astype(dt)
    sizes = (DA_WIDTH,) * 4 + (3 * DN_WIDTH, DN_WIDTH, DN_HEADS, DN_HEADS, D_MODEL)
    da_q, da_k, da_v, da_g, dn_qkv, dn_z, dn_a, dn_b, mg_a, mg_b = jnp.split(
        jnp.dot(h, w_in), np.cumsum(sizes).tolist(), axis=-1)

    lam = (jnp.exp(jnp.sum(lam_q1.astype(F32) * lam_k1.astype(F32)))
           - jnp.exp(jnp.sum(lam_q2.astype(F32) * lam_k2.astype(F32))) + lam_init)
    q = da_q.reshape(B, L, DA_HEADS, 2, DA_HEAD_DIM)
    k_rows = da_k.reshape(B, L, DA_HEADS, 2 * DA_HEAD_DIM)
    v_rows = da_v.reshape(B, L, DA_HEADS, 2 * DA_HEAD_DIM)
    if cache is None:
        o_a = _diff_attn_prompt(q, k_rows.reshape(B, L, DA_HEADS, 2, DA_HEAD_DIM), v_rows, rel_table, lam)
        conv_prev = jnp.zeros((B, CONV_WIDTH - 1, 3 * DN_WIDTH), dt)
        s0 = jnp.zeros((B, DN_HEADS, DN_HEAD_DIM, DN_HEAD_DIM), F32)
    else:
        k_past, v_past, conv_prev, s0 = cache
        P = k_past.shape[1]
        k_all = jnp.concatenate([k_past.astype(dt), k_rows], axis=1)
        v_all = jnp.concatenate([v_past.astype(dt), v_rows], axis=1)
        q_pos = P + jnp.arange(L, dtype=jnp.int32)
        k_pos = jnp.arange(P + L, dtype=jnp.int32)
        o_a = _diff_attend(q, k_all.reshape(B, P + L, DA_HEADS, 2, DA_HEAD_DIM), v_all,
                           q_pos, k_pos, rel_table, lam)
        s0 = s0.astype(F32)
    o_a = _diff_out(o_a, subln_w, lam_init, da_g).astype(dt)

    qkv, conv_tail = _causal_conv(dn_qkv, conv_prev.astype(dt), conv_w)
    q_b, k_b, v_b, beta, g = _gdn_inputs(qkv, dn_a, dn_b, a_log, dt_bias)
    if cache is None:
        n = L // CHUNK
        blk = lambda t: jnp.moveaxis(t.reshape(t.shape[:2] + (n, CHUNK) + t.shape[3:]), 2, 0)
        prep = _gdn_prep(*[blk(t) for t in (q_b, k_b, v_b, beta, g)])
        s_fin, o_b = lax.scan(lambda s, xs: _gdn_step(s, *xs), s0, prep)
        o_b = jnp.moveaxis(o_b, 0, 2).reshape(B, DN_HEADS, L, DN_HEAD_DIM)
    else:
        s_fin, o_b = _gdn_step(s0, *_gdn_prep(q_b, k_b, v_b, beta, g))
    o_b = o_b.transpose(0, 2, 1, 3)
    z = dn_z.reshape(B, L, DN_HEADS, DN_HEAD_DIM).astype(F32)
    o_b = (_rmsnorm(o_b, dn_norm_w, NORM_EPS) * jax.nn.silu(z)).reshape(B, L, DN_WIDTH).astype(dt)

    merged = jax.nn.sigmoid(mg_a) * jnp.dot(o_a, w_pa) + jax.nn.sigmoid(mg_b) * jnp.dot(o_b, w_pb)
    y = jnp.dot(merged, w_out)
    out = _layernorm(DEEPNORM_ALPHA * x + gate * y) * ln_g.astype(F32) + ln_b.astype(F32)
    return out.astype(dt), k_rows, v_rows, conv_tail, s_fin


def setup_inputs(seed: int = 0) -> dict:
    key = jax.random.key(seed)
    ks = jax.random.split(key, 32)
    nrm = lambda k, shape, s: jax.random.normal(k, shape, F32) * s
    col_scale = jnp.concatenate([
        jnp.ones((2 * DA_WIDTH,), F32), jnp.full((DA_WIDTH,), DEEPNORM_BETA, F32),
        jnp.ones((DA_WIDTH + 2 * DN_WIDTH,), F32), jnp.full((DN_WIDTH,), DEEPNORM_BETA, F32),
        jnp.ones((DN_WIDTH + 2 * DN_HEADS + 2 * D_MODEL,), F32)])
    dt0 = jnp.exp(jax.random.uniform(ks[20], (DEPTH, DN_HEADS), F32, math.log(1e-3), math.log(1e-1)))
    return {
        "x_prompt": nrm(ks[0], (BATCH, SEQ, D_MODEL), 1.0),
        "x_sample": nrm(ks[1], (DEC_BATCH, DEC_SEQ, D_MODEL), 1.0),
        "c_prompt": nrm(ks[2], (BATCH, D_MODEL), 1.0),
        "c_sample": nrm(ks[3], (DEC_BATCH, D_MODEL), 1.0),
        "cache_k": nrm(ks[4], (DEPTH, DEC_BATCH, PAST_LEN, DA_HEADS, 2 * DA_HEAD_DIM), 1.0),
        "cache_v": nrm(ks[5], (DEPTH, DEC_BATCH, PAST_LEN, DA_HEADS, 2 * DA_HEAD_DIM), 0.6),
        "state_conv": nrm(ks[6], (DEPTH, DEC_BATCH, CONV_WIDTH - 1, 3 * DN_WIDTH), 1.0),
        "state_delta": nrm(ks[7], (DEPTH, DEC_BATCH, DN_HEADS, DN_HEAD_DIM, DN_HEAD_DIM), 0.3),
        "w_ada": nrm(ks[8], (DEPTH, D_MODEL, 3 * D_MODEL), 0.5 * D_MODEL ** -0.5),
        "b_ada": nrm(ks[9], (DEPTH, 3 * D_MODEL), 0.01),
        "w_in": nrm(ks[10], (DEPTH, D_MODEL, IN_WIDTH), D_MODEL ** -0.5) * col_scale,
        "lam_q1": nrm(ks[11], (DEPTH, DA_HEAD_DIM), 0.1),
        "lam_k1": nrm(ks[12], (DEPTH, DA_HEAD_DIM), 0.1),
        "lam_q2": nrm(ks[13], (DEPTH, DA_HEAD_DIM), 0.1),
        "lam_k2": nrm(ks[14], (DEPTH, DA_HEAD_DIM), 0.1),
        "subln_w": 1.0 + nrm(ks[15], (DEPTH, 2 * DA_HEAD_DIM), 0.02),
        "conv_w": nrm(ks[16], (DEPTH, CONV_WIDTH, 3 * DN_WIDTH), CONV_WIDTH ** -0.5),
        "a_log": jnp.log(jax.random.uniform(ks[17], (DEPTH, DN_HEADS), F32, 1.0, 16.0)),
        "dt_bias": dt0 + jnp.log(-jnp.expm1(-dt0)),
        "dn_norm_w": 1.0 + nrm(ks[18], (DEPTH, DN_HEAD_DIM), 0.02),
        "w_pa": nrm(ks[19], (DEPTH, DA_WIDTH, D_MODEL), DA_WIDTH ** -0.5 * DEEPNORM_BETA),
        "w_pb": nrm(ks[21], (DEPTH, DN_WIDTH, D_MODEL), DN_WIDTH ** -0.5 * DEEPNORM_BETA),
        "w_out": nrm(ks[22], (DEPTH, D_MODEL, D_MODEL), D_MODEL ** -0.5 * DEEPNORM_BETA),
        "ln_g": 1.0 + nrm(ks[23], (DEPTH, D_MODEL), 0.02),
        "ln_b": nrm(ks[24], (DEPTH, D_MODEL), 0.02),
        "rel_table": nrm(ks[25], (REL_BUCKETS, DA_HEADS), 0.5),
    }


def reference(x_prompt, x_sample, c_prompt, c_sample, cache_k, cache_v, state_conv, state_delta,
              w_ada, b_ada, w_in, lam_q1, lam_k1, lam_q2, lam_k2, subln_w, conv_w, a_log, dt_bias,
              dn_norm_w, w_pa, w_pb, w_out, ln_g, ln_b, rel_table):
    y_prompt, y_sample = x_prompt, x_sample
    kp, vp, cp, sp, ksm, vsm, csm, ssm = [], [], [], [], [], [], [], []
    for l in range(DEPTH):
        lam_init = 0.8 - 0.6 * math.exp(-0.3 * l)
        p = (w_ada[l], b_ada[l], w_in[l], lam_q1[l], lam_k1[l], lam_q2[l], lam_k2[l], subln_w[l],
             conv_w[l], a_log[l], dt_bias[l], dn_norm_w[l], w_pa[l], w_pb[l], w_out[l], ln_g[l], ln_b[l])
        y_prompt, k_, v_, c_, s_ = _layer(y_prompt, c_prompt, p, rel_table, lam_init, None)
        kp.append(k_); vp.append(v_); cp.append(c_); sp.append(s_)
        y_sample, k_, v_, c_, s_ = _layer(y_sample, c_sample, p, rel_table, lam_init,
                                          (cache_k[l], cache_v[l], state_conv[l], state_delta[l]))
        ksm.append(k_); vsm.append(v_); csm.append(c_); ssm.append(s_)
    return (y_prompt, y_sample, jnp.stack(kp), jnp.stack(vp), jnp.stack(cp), jnp.stack(sp),
            jnp.stack(ksm), jnp.stack(vsm), jnp.stack(csm), jnp.stack(ssm))
```

```python
import functools
import math

import jax
import jax.numpy as jnp
import numpy as np
from jax import lax
from jax.experimental import pallas as pl
from jax.experimental.pallas import tpu as pltpu

F32 = jnp.float32
BF16 = jnp.bfloat16
HIGHEST = lax.Precision.HIGHEST

CHUNK = 64
DA_HEADS = 4
DA_HEAD_DIM = 64
DA_VDIM = 2 * DA_HEAD_DIM
DA_WIDTH = DA_HEADS * DA_VDIM
DN_HEADS = 4
DN_HEAD_DIM = 128
DN_WIDTH = DN_HEADS * DN_HEAD_DIM
CONV_WIDTH = 4
REL_BUCKETS = 32
REL_MAX_DIST = 128
LN_EPS = 1e-5
NORM_EPS = 1e-6

LANES = 128
AB_PAD = LANES
CONV_HALO = 8
VMEM_LIMIT = 56 * 1024 * 1024

_OFF_Q = 0
_OFF_K = _OFF_Q + DA_WIDTH
_OFF_V = _OFF_K + DA_WIDTH
_OFF_G = _OFF_V + DA_WIDTH
_OFF_QKV = _OFF_G + DA_WIDTH
_OFF_Z = _OFF_QKV + 3 * DN_WIDTH
_OFF_AB = _OFF_Z + DN_WIDTH
_OFF_M = _OFF_AB + AB_PAD


def _params(sem):
    return pltpu.CompilerParams(dimension_semantics=sem, vmem_limit_bytes=VMEM_LIMIT)


def _sigmoid(x):
    return 1.0 / (1.0 + jnp.exp(-x))


def _silu(x):
    return x * _sigmoid(x)


def _dot(a, b, **kw):
    return jnp.dot(a, b, preferred_element_type=F32, **kw)


def _dot_nt(a, b, **kw):
    return lax.dot_general(a, b, (((1,), (1,)), ((), ())), preferred_element_type=F32, **kw)


def _dot_tn(a, b, **kw):
    return lax.dot_general(a, b, (((0,), (0,)), ((), ())), preferred_element_type=F32, **kw)


def _mod_kernel(c_ref, w_ref, b_ref, o_ref):
    o_ref[...] = _dot(_silu(c_ref[...]), w_ref[...], precision=HIGHEST) + b_ref[...]


def _mod(c, w_ada, b_ada):
    n, d = c.shape
    d3 = w_ada.shape[1]
    tn = 512
    return pl.pallas_call(
        _mod_kernel,
        out_shape=jax.ShapeDtypeStruct((n, d3), F32),
        grid=(d3 // tn,),
        in_specs=[pl.BlockSpec((n, d), lambda j: (0, 0)),
                  pl.BlockSpec((d, tn), lambda j: (0, j)),
                  pl.BlockSpec((1, tn), lambda j: (0, j))],
        out_specs=pl.BlockSpec((n, tn), lambda j: (0, j)),
        compiler_params=_params(("arbitrary",)),
        name="mod",
    )(c, w_ada, b_ada.reshape(1, d3))


def _proj_kernel(x_ref, shift_ref, scale_ref, w_ref, convw_ref, convprev_ref, alog_ref, dtb_ref,
                 q_ref, k_ref, v_ref, kb_ref, vb_ref, g_ref, qkv_ref, z_ref, gb_ref, m_ref, tail_ref,
                 buf_ref, *, tm, d_model):
    t = pl.program_id(1)
    x = x_ref[0]
    mu = jnp.mean(x, axis=-1, keepdims=True)
    xc = x - mu
    var = jnp.mean(xc * xc, axis=-1, keepdims=True)
    ln = xc * lax.rsqrt(var + LN_EPS)
    h = (ln * (1.0 + scale_ref[0]) + shift_ref[0]).astype(BF16)

    def seg(a, b):
        return _dot(h, w_ref[:, a:b])

    q_ref[0] = (seg(_OFF_Q, _OFF_K) * (DA_HEAD_DIM ** -0.5)).astype(BF16)
    kk = seg(_OFF_K, _OFF_V)
    k_ref[0] = kk
    kb_ref[0] = kk.astype(BF16)
    vv = seg(_OFF_V, _OFF_G)
    v_ref[0] = vv
    vb_ref[0] = vv.astype(BF16)
    g_ref[0] = _silu(seg(_OFF_G, _OFF_QKV)).astype(BF16)
    z_ref[0] = _silu(seg(_OFF_Z, _OFF_AB)).astype(BF16)
    m_ref[0] = _sigmoid(seg(_OFF_M, _OFF_M + 2 * d_model)).astype(BF16)

    ab = seg(_OFF_AB, _OFF_M)
    sp_in = ab + dtb_ref[...]
    softplus = jnp.maximum(sp_in, 0.0) + jnp.log1p(jnp.exp(-jnp.abs(sp_in)))
    gdec = -jnp.exp(alog_ref[...]) * softplus
    lane = lax.broadcasted_iota(jnp.int32, ab.shape, 1)
    gb_ref[0] = jnp.where(lane < DN_HEADS, gdec, _sigmoid(ab))

    xq = seg(_OFF_QKV, _OFF_Z)
    lo = CONV_HALO - (CONV_WIDTH - 1)

    @pl.when(t == 0)
    def _():
        buf_ref[lo:CONV_HALO, :] = convprev_ref[0]

    buf_ref[CONV_HALO:CONV_HALO + tm, :] = xq
    cw = convw_ref[...]
    y = buf_ref[lo:lo + tm, :] * cw[0:1]
    for j in range(1, CONV_WIDTH - 1):
        y = y + buf_ref[lo + j:lo + j + tm, :] * cw[j:j + 1]
    y = y + xq * cw[CONV_WIDTH - 1:CONV_WIDTH]
    tail = buf_ref[tm + lo:tm + CONV_HALO, :]
    tail_ref[0] = tail
    buf_ref[lo:CONV_HALO, :] = tail
    y = _silu(y)

    for hh in range(DN_HEADS):
        a, b = hh * DN_HEAD_DIM, (hh + 1) * DN_HEAD_DIM
        qh = y[:, a:b]
        qkv_ref[0, :, a:b] = qh * lax.rsqrt(jnp.sum(qh * qh, axis=-1, keepdims=True) + NORM_EPS) * (DN_HEAD_DIM ** -0.5)
        kh = y[:, DN_WIDTH + a:DN_WIDTH + b]
        qkv_ref[0, :, DN_WIDTH + a:DN_WIDTH + b] = kh * lax.rsqrt(jnp.sum(kh * kh, axis=-1, keepdims=True) + NORM_EPS)
    qkv_ref[0, :, 2 * DN_WIDTH:] = y[:, 2 * DN_WIDTH:]


def _proj(x, shift, scale, w_cat, conv_w, conv_prev, alog_row, dtb_row, tm):
    bsz, seq, d = x.shape
    n_cat = w_cat.shape[1]
    nt = seq // tm
    kern = functools.partial(_proj_kernel, tm=tm, d_model=d)
    tok = lambda w: pl.BlockSpec((1, tm, w), lambda b, t: (b, t, 0))
    per_b = lambda r, w: pl.BlockSpec((1, r, w), lambda b, t: (b, 0, 0))
    const = lambda r, w: pl.BlockSpec((r, w), lambda b, t: (0, 0))
    out_shape = (
        jax.ShapeDtypeStruct((bsz, seq, DA_WIDTH), BF16),
        jax.ShapeDtypeStruct((bsz, seq, DA_WIDTH), F32),
        jax.ShapeDtypeStruct((bsz, seq, DA_WIDTH), F32),
        jax.ShapeDtypeStruct((bsz, seq, DA_WIDTH), BF16),
        jax.ShapeDtypeStruct((bsz, seq, DA_WIDTH), BF16),
        jax.ShapeDtypeStruct((bsz, seq, DA_WIDTH), BF16),
        jax.ShapeDtypeStruct((bsz, seq, 3 * DN_WIDTH), F32),
        jax.ShapeDtypeStruct((bsz, seq, DN_WIDTH), BF16),
        jax.ShapeDtypeStruct((bsz, seq, AB_PAD), F32),
        jax.ShapeDtypeStruct((bsz, seq, 2 * d), BF16),
        jax.ShapeDtypeStruct((bsz, CONV_WIDTH - 1, 3 * DN_WIDTH), F32),
    )
    out_specs = (tok(DA_WIDTH), tok(DA_WIDTH), tok(DA_WIDTH), tok(DA_WIDTH), tok(DA_WIDTH), tok(DA_WIDTH),
                 tok(3 * DN_WIDTH), tok(DN_WIDTH), tok(AB_PAD), tok(2 * d),
                 per_b(CONV_WIDTH - 1, 3 * DN_WIDTH))
    return pl.pallas_call(
        kern,
        out_shape=out_shape,
        grid=(bsz, nt),
        in_specs=[tok(d), per_b(1, d), per_b(1, d), const(d, n_cat), const(CONV_WIDTH, 3 * DN_WIDTH),
                  per_b(CONV_WIDTH - 1, 3 * DN_WIDTH), const(1, AB_PAD), const(1, AB_PAD)],
        out_specs=out_specs,
        scratch_shapes=[pltpu.VMEM((tm + CONV_HALO, 3 * DN_WIDTH), F32)],
        compiler_params=_params(("arbitrary", "arbitrary")),
        name="proj",
    )(x, shift, scale, w_cat, conv_w, conv_prev, alog_row, dtb_row)


def _bias_kernel(tab_ref, bkt_ref, o_ref, *, n_tiles):
    h = pl.program_id(0)
    far = tab_ref[REL_BUCKETS // 2 - 1, h]
    for t in range(n_tiles):
        b = bkt_ref[t]
        acc = jnp.full(b.shape, -jnp.inf, F32)
        for i in range(REL_BUCKETS):
            acc = jnp.where(b == i, tab_ref[i, h] - far, acc)
        o_ref[0, t] = acc


def _bias_tiles(rel_table, buckets):
    n, r, c = buckets.shape
    return pl.pallas_call(
        functools.partial(_bias_kernel, n_tiles=n),
        out_shape=jax.ShapeDtypeStruct((DA_HEADS, n, r, c), F32),
        grid_spec=pltpu.PrefetchScalarGridSpec(
            num_scalar_prefetch=0,
            grid=(DA_HEADS,),
            in_specs=[pl.BlockSpec(memory_space=pltpu.SMEM),
                      pl.BlockSpec((n, r, c), lambda h: (0, 0, 0))],
            out_specs=pl.BlockSpec((1, n, r, c), lambda h: (h, 0, 0, 0)),
        ),
        compiler_params=_params(("arbitrary",)),
        name="bias",
    )(rel_table, buckets)


def _rel_bucket(rel):
    nb = REL_BUCKETS // 2
    max_exact = nb // 2
    n = jnp.abs(rel)
    large = max_exact + (jnp.log(jnp.maximum(n, 1).astype(F32) / max_exact)
                         / math.log(REL_MAX_DIST / max_exact) * (nb - max_exact)).astype(jnp.int32)
    large = jnp.minimum(large, nb - 1)
    return jnp.where(rel > 0, nb, 0) + jnp.where(n < max_exact, n, large)


def _bucket_map(q_pos, k_pos):
    rel = k_pos[None, :] - q_pos[:, None]
    vis = (k_pos[None, :] // CHUNK) <= (q_pos[:, None] // CHUNK)
    return jnp.where(vis, _rel_bucket(rel), -1).astype(jnp.int32)


def _stack_maps(q):
    lane = lax.broadcasted_iota(jnp.int32, q.shape, 1)
    zero = jnp.zeros_like(q)
    return jnp.concatenate([jnp.where(lane < DA_HEAD_DIM, q, zero), jnp.where(lane >= DA_HEAD_DIM, q, zero)], axis=0)


def _add_bias(s, bias):
    t, kk = bias.shape
    return (s.reshape(2, t, kk) + bias[None]).reshape(2 * t, kk)


def _lambda(lamv_ref, lam_init):
    lv = lamv_ref[...]
    s1 = jnp.sum(lv[0:1] * lv[1:2], axis=-1, keepdims=True)
    s2 = jnp.sum(lv[2:3] * lv[3:4], axis=-1, keepdims=True)
    return jnp.exp(s1) - jnp.exp(s2) + lam_init


def _diff_finalize(acc, l, lam, subw, gate, lam_init):
    t = acc.shape[0] // 2
    o = acc / l
    od = o[:t] - lam * o[t:]
    ms = jnp.mean(od * od, axis=-1, keepdims=True)
    return od * lax.rsqrt(ms + LN_EPS) * subw * (1.0 - lam_init) * gate


def _attn_kernel(q_ref, k_ref, v_ref, g_ref, bias_ref, lamv_ref, subw_ref, o_ref, *, t, lam_init):
    qi = pl.program_id(2)
    qs = _stack_maps(q_ref[0])

    def block(j):
        off = pl.multiple_of(j * t, t)
        return k_ref[0, pl.ds(off, t), :], v_ref[0, pl.ds(off, t), :]

    def update(carry, s, vblk):
        m, l, acc = carry
        m_new = jnp.maximum(m, jnp.max(s, axis=-1, keepdims=True))
        alpha = jnp.exp(m - m_new)
        p = jnp.exp(s - m_new)
        l = alpha * l + jnp.sum(p, axis=-1, keepdims=True)
        acc = alpha * acc + _dot(p.astype(BF16), vblk)
        return m_new, l, acc

    kd, vd = block(qi)
    s = _add_bias(_dot_nt(qs, kd), bias_ref[0, 0])
    m0 = jnp.max(s, axis=-1, keepdims=True)
    p = jnp.exp(s - m0)
    carry = (m0, jnp.sum(p, axis=-1, keepdims=True), _dot(p.astype(BF16), vd))

    def adjacent(c):
        ka, va = block(qi - 1)
        return update(c, _add_bias(_dot_nt(qs, ka), bias_ref[0, 1]), va)

    carry = lax.cond(qi > 0, adjacent, lambda c: c, carry)

    def far(j, c):
        kf, vf = block(j)
        return update(c, _dot_nt(qs, kf), vf)

    m, l, acc = lax.fori_loop(0, jnp.maximum(qi - 1, 0), far, carry)
    out = _diff_finalize(acc, l, _lambda(lamv_ref, lam_init), subw_ref[...], g_ref[0].astype(F32), lam_init)
    o_ref[0] = out.astype(BF16)


def _attn_prompt(q, kb, vb, g, bias, lamv, subw, lam_init, t):
    bsz, seq, _ = q.shape
    nq = seq // t
    tile = lambda: pl.BlockSpec((1, t, DA_VDIM), lambda b, h, i: (b, i, h))
    full = lambda: pl.BlockSpec((1, seq, DA_VDIM), lambda b, h, i: (b, 0, h))
    return pl.pallas_call(
        functools.partial(_attn_kernel, t=t, lam_init=lam_init),
        out_shape=jax.ShapeDtypeStruct((bsz, seq, DA_WIDTH), BF16),
        grid=(bsz, DA_HEADS, nq),
        in_specs=[tile(), full(), full(), tile(),
                  pl.BlockSpec((1, 2, t, t), lambda b, h, i: (h, 0, 0, 0)),
                  pl.BlockSpec((8, LANES), lambda b, h, i: (0, 0)),
                  pl.BlockSpec((1, DA_VDIM), lambda b, h, i: (0, 0))],
        out_specs=tile(),
        compiler_params=_params(("arbitrary", "arbitrary", "arbitrary")),
        name="attn_prompt",
    )(q, kb, vb, g, bias, lamv, subw)


def _attn_cache_kernel(q_ref, kc_ref, vc_ref, kn_ref, vn_ref, g_ref, bias_ref, lamv_ref, subw_ref, o_ref,
                       *, past, lam_init):
    lam = _lambda(lamv_ref, lam_init)
    for h in range(DA_HEADS):
        a, b = h * DA_VDIM, (h + 1) * DA_VDIM
        qs = _stack_maps(q_ref[0, :, a:b])
        kc = kc_ref[0, :, a:b].astype(BF16)
        vc = vc_ref[0, :, a:b].astype(BF16)
        kn = kn_ref[0, :, a:b]
        vn = vn_ref[0, :, a:b]
        bias = bias_ref[h, 0]
        sp = _add_bias(_dot_nt(qs, kc), bias[:, :past])
        sn = _add_bias(_dot_nt(qs, kn), bias[:, past:])
        m = jnp.maximum(jnp.max(sp, axis=-1, keepdims=True), jnp.max(sn, axis=-1, keepdims=True))
        pp = jnp.exp(sp - m)
        pn = jnp.exp(sn - m)
        l = jnp.sum(pp, axis=-1, keepdims=True) + jnp.sum(pn, axis=-1, keepdims=True)
        acc = _dot(pp.astype(BF16), vc) + _dot(pn.astype(BF16), vn)
        out = _diff_finalize(acc, l, lam, subw_ref[...], g_ref[0, :, a:b].astype(F32), lam_init)
        o_ref[0, :, a:b] = out.astype(BF16)


def _attn_sample(q, cache_k, cache_v, kb, vb, g, bias, lamv, subw, lam_init):
    bsz, ln, _ = q.shape
    past = cache_k.shape[1]
    new = lambda: pl.BlockSpec((1, ln, DA_WIDTH), lambda b: (b, 0, 0))
    old = lambda: pl.BlockSpec((1, past, DA_WIDTH), lambda b: (b, 0, 0))
    return pl.pallas_call(
        functools.partial(_attn_cache_kernel, past=past, lam_init=lam_init),
        out_shape=jax.ShapeDtypeStruct((bsz, ln, DA_WIDTH), BF16),
        grid=(bsz,),
        in_specs=[new(), old(), old(), new(), new(), new(),
                  pl.BlockSpec((DA_HEADS, 1, ln, past + ln), lambda b: (0, 0, 0, 0)),
                  pl.BlockSpec((8, LANES), lambda b: (0, 0)),
                  pl.BlockSpec((1, DA_VDIM), lambda b: (0, 0))],
        out_specs=new(),
        compiler_params=_params(("arbitrary",)),
        name="attn_sample",
    )(q, cache_k, cache_v, kb, vb, g, bias, lamv, subw)


def _bmm(a, b):
    return jnp.einsum('bik,bkj->bij', a, b, preferred_element_type=F32)


def _bmm_nt(a, b):
    return jnp.einsum('bik,bjk->bij', a, b, preferred_element_type=F32)


def _split3(x):
    hi = x.astype(BF16)
    r = x - hi.astype(F32)
    mid = r.astype(BF16)
    lo = (r - mid.astype(F32)).astype(BF16)
    return hi, mid, lo


def _split2(x):
    hi = x.astype(BF16)
    return hi, (x - hi.astype(F32)).astype(BF16)


def _gdn_prep_kernel(qkv_ref, gb_ref, u_ref, w_ref, qg_ref, kdt_ref, qk_ref, gc_ref, *, c, nc):
    lg = c * nc
    gbv = gb_ref[0]

    row = lax.broadcasted_iota(jnp.int32, (lg, lg), 0)
    col = lax.broadcasted_iota(jnp.int32, (lg, lg), 1)
    tril_bd = jnp.where((row // c == col // c) & (row >= col), 1.0, 0.0).astype(BF16)
    gcum = sum(_dot(tril_bd, part) for part in _split3(gbv))
    gc_ref[0] = gcum
    pick = jnp.where(lax.broadcasted_iota(jnp.int32, (8, LANES), 0)
                     == lax.broadcasted_iota(jnp.int32, (8, LANES), 1), 1.0, 0.0).astype(BF16)
    gcum_t = sum(_dot_nt(pick, part) for part in _split3(gcum))

    pairs = [(ci, h) for ci in range(nc) for h in range(DN_HEADS)]

    def heads(base):
        return jnp.stack([qkv_ref[0, ci * c:(ci + 1) * c, base + h * DN_HEAD_DIM:base + (h + 1) * DN_HEAD_DIM]
                          for ci, h in pairs])

    q, k, v = heads(0), heads(DN_WIDTH), heads(2 * DN_WIDTH)
    beta = jnp.stack([gbv[ci * c:(ci + 1) * c, DN_HEADS + h:DN_HEADS + h + 1] for ci, h in pairs])
    gi = jnp.stack([gcum[ci * c:(ci + 1) * c, h:h + 1] for ci, h in pairs])
    gj = jnp.stack([gcum_t[h:h + 1, ci * c:(ci + 1) * c] for ci, h in pairs])
    glast = jnp.stack([gcum[(ci + 1) * c - 1:(ci + 1) * c, h:h + 1] for ci, h in pairs])

    r3 = lax.broadcasted_iota(jnp.int32, (1, c, c), 1)
    c3 = lax.broadcasted_iota(jnp.int32, (1, c, c), 2)
    decay = jnp.exp(jnp.where(r3 >= c3, gi - gj, -jnp.inf))
    kbeta = k * beta
    k16 = k.astype(BF16)
    lower = jnp.where(r3 > c3, _bmm_nt(kbeta.astype(BF16), k16) * decay, 0.0)

    n0 = min(8, c)
    x = jnp.where(r3 // n0 == c3 // n0, -lower, 0.0)
    toff = x
    for _ in range(int(math.log2(n0)) - 1):
        x16 = x.astype(BF16)
        x = _bmm(x16, x16)
        toff = toff + x + _bmm(toff.astype(BF16), x.astype(BF16))
    eye = jnp.where(r3 == c3, 1.0, 0.0)
    n = n0
    while n < c:
        loff = jnp.where((r3 // (2 * n) == c3 // (2 * n)) & (r3 // n != c3 // n), lower, 0.0).astype(BF16)
        t16 = (eye + toff).astype(BF16)
        toff = toff - _bmm(t16, _bmm(loff, t16).astype(BF16))
        n *= 2
    t16 = toff.astype(BF16)

    eg = jnp.exp(gi)
    rhs = jnp.concatenate([v * beta, kbeta * eg], axis=2)
    sol = rhs + _bmm(t16, rhs.astype(BF16))
    l_hi, l_lo = _split2(lower)
    s_hi, s_lo = _split2(sol)
    resid = (rhs - sol) - (_bmm(l_hi, s_hi) + (_bmm(l_hi, s_lo) + _bmm(l_lo, s_hi)))
    sol = sol + (resid + _bmm(t16, resid.astype(BF16)))
    u, w = sol[:, :, :DN_HEAD_DIM], sol[:, :, DN_HEAD_DIM:]

    qk = _bmm_nt(q.astype(BF16), k16) * decay
    qk_pad = jnp.concatenate([qk, jnp.zeros((len(pairs), c, DN_HEAD_DIM - c), F32)], axis=2).astype(BF16)
    qg = (q * eg).astype(BF16)
    kdec = (k * jnp.exp(glast - gi)).astype(BF16)
    eye_d = jnp.where(lax.broadcasted_iota(jnp.int32, (1, DN_HEAD_DIM, DN_HEAD_DIM), 1)
                      == lax.broadcasted_iota(jnp.int32, (1, DN_HEAD_DIM, DN_HEAD_DIM), 2), 1.0, 0.0).astype(BF16)
    kdt = _bmm_nt(jnp.broadcast_to(eye_d, (len(pairs), DN_HEAD_DIM, DN_HEAD_DIM)), kdec)
    kdt_pad = jnp.concatenate([kdt, jnp.zeros((len(pairs), DN_HEAD_DIM, LANES - c), F32)], axis=2).astype(BF16)
    w16 = w.astype(BF16)
    for i, (ci, h) in enumerate(pairs):
        rows = slice(ci * c, (ci + 1) * c)
        cols = slice(h * DN_HEAD_DIM, (h + 1) * DN_HEAD_DIM)
        u_ref[0, rows, cols] = u[i]
        w_ref[0, rows, cols] = w16[i]
        qg_ref[0, rows, cols] = qg[i]
        qk_ref[0, rows, cols] = qk_pad[i]
        kdt_ref[0, ci, cols, :] = kdt_pad[i]


def _gdn_prep(qkv, gb, c, nc):
    bsz, seq, _ = qkv.shape
    lg = c * nc
    tok = lambda w: pl.BlockSpec((1, lg, w), lambda b, t: (b, t, 0))
    tok_shape = lambda dt: jax.ShapeDtypeStruct((bsz, seq, DN_WIDTH), dt)
    return pl.pallas_call(
        functools.partial(_gdn_prep_kernel, c=c, nc=nc),
        out_shape=(tok_shape(F32), tok_shape(BF16), tok_shape(BF16),
                   jax.ShapeDtypeStruct((bsz, seq // c, DN_WIDTH, LANES), BF16),
                   tok_shape(BF16),
                   jax.ShapeDtypeStruct((bsz, seq, AB_PAD), F32)),
        grid=(bsz, seq // lg),
        in_specs=[tok(3 * DN_WIDTH), tok(AB_PAD)],
        out_specs=(tok(DN_WIDTH), tok(DN_WIDTH), tok(DN_WIDTH),
                   pl.BlockSpec((1, nc, DN_WIDTH, LANES), lambda b, t: (b, t, 0, 0)),
                   tok(DN_WIDTH), tok(AB_PAD)),
        compiler_params=_params(("arbitrary", "arbitrary")),
        name="gdn_prep",
    )(qkv, gb)


def _gdn_scan_kernel(u_ref, w_ref, qg_ref, kdt_ref, qk_ref, gc_ref, z_ref, s0_ref, nw_ref, o_ref, sfin_ref, s_scr,
                     *, c, bg):
    t = pl.program_id(1)
    nb = bg * DN_HEADS

    @pl.when(t == 0)
    def _():
        s_scr[...] = s0_ref[...].reshape(nb, DN_HEAD_DIM, DN_HEAD_DIM)

    pairs = [(b, h) for b in range(bg) for h in range(DN_HEADS)]

    def heads(ref):
        return jnp.stack([ref[b, :, h * DN_HEAD_DIM:(h + 1) * DN_HEAD_DIM] for b, h in pairs])

    u, w, qg = heads(u_ref), heads(w_ref), heads(qg_ref)
    qk = heads(qk_ref)[:, :, :c]
    kdt = jnp.stack([kdt_ref[b, 0, h * DN_HEAD_DIM:(h + 1) * DN_HEAD_DIM, :] for b, h in pairs])[:, :, :c]
    glast = jnp.stack([gc_ref[b, c - 1:c, h:h + 1] for b, h in pairs])
    s = s_scr[...]
    s16 = s.astype(BF16)
    ws = _bmm(jnp.concatenate([w, qg], axis=1), s16)
    v_new = u - ws[:, :c]
    vn16 = v_new.astype(BF16)
    o = ws[:, c:] + _bmm(qk, vn16)
    s_new = s * jnp.exp(glast) + _bmm(kdt, vn16)
    s_scr[...] = s_new
    on = o * lax.rsqrt(jnp.mean(o * o, axis=-1, keepdims=True) + NORM_EPS) * nw_ref[...]
    for i, (b, h) in enumerate(pairs):
        cols = slice(h * DN_HEAD_DIM, (h + 1) * DN_HEAD_DIM)
        o_ref[b, :, cols] = (on[i] * z_ref[b, :, cols].astype(F32)).astype(BF16)
    sfin_ref[...] = s_new.reshape(bg, DN_HEADS, DN_HEAD_DIM, DN_HEAD_DIM)


def _gdn_scan(u, w, qg, kdt, qk, gc, z, s0, nw, c, bg):
    bsz, seq, _ = u.shape
    tok = lambda wd: pl.BlockSpec((bg, c, wd), lambda g, t: (g, t, 0))
    st = lambda: pl.BlockSpec((bg, DN_HEADS, DN_HEAD_DIM, DN_HEAD_DIM), lambda g, t: (g, 0, 0, 0))
    return pl.pallas_call(
        functools.partial(_gdn_scan_kernel, c=c, bg=bg),
        out_shape=(jax.ShapeDtypeStruct((bsz, seq, DN_WIDTH), BF16),
                   jax.ShapeDtypeStruct((bsz, DN_HEADS, DN_HEAD_DIM, DN_HEAD_DIM), F32)),
        grid=(bsz // bg, seq // c),
        in_specs=[tok(DN_WIDTH), tok(DN_WIDTH), tok(DN_WIDTH),
                  pl.BlockSpec((bg, 1, DN_WIDTH, LANES), lambda g, t: (g, t, 0, 0)),
                  tok(DN_WIDTH), tok(AB_PAD), tok(DN_WIDTH), st(),
                  pl.BlockSpec((1, DN_HEAD_DIM), lambda g, t: (0, 0))],
        out_specs=(tok(DN_WIDTH), st()),
        scratch_shapes=[pltpu.VMEM((bg * DN_HEADS, DN_HEAD_DIM, DN_HEAD_DIM), F32)],
        compiler_params=_params(("arbitrary", "arbitrary")),
        name="gdn_scan",
    )(u, w, qg, kdt, qk, gc, z, s0, nw)


def _gdn(qkv, gb, z, s0, nw, c, nc):
    bsz = qkv.shape[0]
    u, w, qg, kdt, qk, gc = _gdn_prep(qkv, gb, c, nc)
    return _gdn_scan(u, w, qg, kdt, qk, gc, z, s0, nw, c, _pick_tile(bsz, 8))


def _merge_kernel(x_ref, oa_ref, ob_ref, m_ref, gate_ref, wpa_ref, wpb_ref, wout_ref, lng_ref, lnb_ref, y_ref,
                  *, alpha, d_model):
    pa = _dot(oa_ref[0], wpa_ref[...])
    pb = _dot(ob_ref[0], wpb_ref[...])
    merged = m_ref[0, :, :d_model].astype(F32) * pa + m_ref[0, :, d_model:].astype(F32) * pb
    y = _dot(merged.astype(BF16), wout_ref[...])
    r = alpha * x_ref[0] + gate_ref[0] * y
    mu = jnp.mean(r, axis=-1, keepdims=True)
    rc = r - mu
    var = jnp.mean(rc * rc, axis=-1, keepdims=True)
    y_ref[0] = rc * lax.rsqrt(var + LN_EPS) * lng_ref[...] + lnb_ref[...]


def _merge(x, oa, ob, m, gate, wpa, wpb, wout, lng, lnb, alpha, tm):
    bsz, seq, d = x.shape
    tok = lambda w: pl.BlockSpec((1, tm, w), lambda b, t: (b, t, 0))
    const = lambda r, w: pl.BlockSpec((r, w), lambda b, t: (0, 0))
    return pl.pallas_call(
        functools.partial(_merge_kernel, alpha=alpha, d_model=d),
        out_shape=jax.ShapeDtypeStruct((bsz, seq, d), F32),
        grid=(bsz, seq // tm),
        in_specs=[tok(d), tok(DA_WIDTH), tok(DN_WIDTH), tok(2 * d),
                  pl.BlockSpec((1, 1, d), lambda b, t: (b, 0, 0)),
                  const(DA_WIDTH, d), const(DN_WIDTH, d), const(d, d), const(1, d), const(1, d)],
        out_specs=tok(d),
        compiler_params=_params(("arbitrary", "arbitrary")),
        name="merge",
    )(x, oa, ob, m, gate, wpa, wpb, wout, lng, lnb)


def _pick_tile(n, target):
    t = min(n, target)
    while n % t:
        t //= 2
    return t


def _layer(x, mod, p, rel_table, lam_init, alpha, cache):
    (w_cat, lamv, subw, conv_w, alog_row, dtb_row, nw, wpa, wpb, wout, lng, lnb) = p
    bsz, seq, d = x.shape
    shift, scale, gate = (mod[:, None, i * d:(i + 1) * d] for i in range(3))
    if cache is None:
        conv_prev = jnp.zeros((bsz, CONV_WIDTH - 1, 3 * DN_WIDTH), F32)
        s0 = jnp.zeros((bsz, DN_HEADS, DN_HEAD_DIM, DN_HEAD_DIM), F32)
    else:
        k_past, v_past, conv_prev, s0 = cache
    tm = _pick_tile(seq, 256)
    q, k, v, kb, vb, g, qkv, z, gb, m, tail = _proj(x, shift, scale, w_cat, conv_w, conv_prev, alog_row, dtb_row, tm)

    if cache is None:
        t = _pick_tile(seq, 256)
        assert t >= REL_MAX_DIST and t % CHUNK == 0
        pos = jnp.arange(t, dtype=jnp.int32)
        buckets = jnp.stack([_bucket_map(pos, pos), _bucket_map(pos + t, pos)])
        bias = _bias_tiles(rel_table, buckets)
        oa = _attn_prompt(q, kb, vb, g, bias, lamv, subw, lam_init, t)
        c = CHUNK
        nc = _pick_tile(seq // c, 4)
    else:
        past = k_past.shape[1]
        q_pos = past + jnp.arange(seq, dtype=jnp.int32)
        k_pos = jnp.arange(past + seq, dtype=jnp.int32)
        bias = _bias_tiles(rel_table, _bucket_map(q_pos, k_pos)[None])
        oa = _attn_sample(q, k_past.reshape(bsz, past, DA_WIDTH), v_past.reshape(bsz, past, DA_WIDTH),
                          kb, vb, g, bias, lamv, subw, lam_init)
        c, nc = seq, 1
    ob, s_fin = _gdn(qkv, gb, z, s0.astype(F32), nw, c, nc)
    y = _merge(x, oa, ob, m, gate, wpa, wpb, wout, lng, lnb, alpha, tm)
    k_rows = k.reshape(bsz, seq, DA_HEADS, DA_VDIM)
    v_rows = v.reshape(bsz, seq, DA_HEADS, DA_VDIM)
    return y, k_rows, v_rows, tail, s_fin


def _pad_row(v, width):
    return jnp.zeros((1, width), F32).at[0, :v.shape[0]].set(v.astype(F32))


def kernel(x_prompt, x_sample, c_prompt, c_sample, cache_k, cache_v, state_conv, state_delta, w_ada, b_ada, w_in, lam_q1, lam_k1, lam_q2, lam_k2, subln_w, conv_w, a_log, dt_bias, dn_norm_w, w_pa, w_pb, w_out, ln_g, ln_b, rel_table):
    depth = w_ada.shape[0]
    d = x_prompt.shape[-1]
    alpha = (2.0 * depth) ** 0.25
    nb = x_prompt.shape[0]
    y_prompt, y_sample = x_prompt, x_sample
    outs = [[] for _ in range(8)]
    c_all = jnp.concatenate([c_prompt, c_sample], axis=0)
    n_ab = 2 * DN_HEADS
    for l in range(depth):
        lam_init = 0.8 - 0.6 * math.exp(-0.3 * l)
        wi = w_in[l]
        w_cat = jnp.concatenate([wi[:, :_OFF_AB], wi[:, _OFF_AB:_OFF_AB + n_ab],
                                 jnp.zeros((d, AB_PAD - n_ab), wi.dtype), wi[:, _OFF_AB + n_ab:]], axis=1).astype(BF16)
        lamv = jnp.zeros((8, LANES), F32)
        for i, lv in enumerate((lam_q1[l], lam_k1[l], lam_q2[l], lam_k2[l])):
            lamv = lamv.at[i, :DA_HEAD_DIM].set(lv.astype(F32))
        p = (w_cat, lamv, subln_w[l].reshape(1, DA_VDIM).astype(F32), conv_w[l].astype(F32),
             _pad_row(a_log[l], AB_PAD), _pad_row(dt_bias[l], AB_PAD), dn_norm_w[l].reshape(1, DN_HEAD_DIM).astype(F32),
             w_pa[l].astype(BF16), w_pb[l].astype(BF16), w_out[l].astype(BF16),
             ln_g[l].reshape(1, d).astype(F32), ln_b[l].reshape(1, d).astype(F32))
        mod = _mod(c_all, w_ada[l], b_ada[l])
        y_prompt, k_, v_, c_, s_ = _layer(y_prompt, mod[:nb], p, rel_table, lam_init, alpha, None)
        for o, val in zip(outs[:4], (k_, v_, c_, s_)):
            o.append(val)
        y_sample, k_, v_, c_, s_ = _layer(y_sample, mod[nb:], p, rel_table, lam_init, alpha,
                                          (cache_k[l], cache_v[l], state_conv[l], state_delta[l]))
        for o, val in zip(outs[4:], (k_, v_, c_, s_)):
            o.append(val)
    return (y_prompt, y_sample) + tuple(jnp.stack(o) for o in outs)
```
